```python
import jax, jax.numpy as jnp
from jax import lax
import numpy as np

D_MODEL = 2048
BATCH = 4
SEQ = 4096
DEPTH = 1

GRID_W = 64
CTX_LEN = 256
RWKV_WIDTH = 1024
RWKV_HEAD = 64
RWKV_HEADS = RWKV_WIDTH // RWKV_HEAD
LORA = 64
GN_EPS = 64e-5
ATT_WIDTH = D_MODEL - RWKV_WIDTH
HEAD_DIM = 64
N_HEADS = ATT_WIDTH // HEAD_DIM
KV_HEADS = 4
GROUP = N_HEADS // KV_HEADS
KV_WIDTH = KV_HEADS * HEAD_DIM
WINDOW = 128
BLOCK = 128
ROPE_THETA = 10000.0
NORM_EPS = 1e-6
SHIFT_COLS = 3 * RWKV_WIDTH + 4 * LORA
IN_COLS = SHIFT_COLS + RWKV_WIDTH + ATT_WIDTH + 2 * KV_WIDTH + ATT_WIDTH

kernel_name = "hymba_rwkv7_swa_prefix_dit_block"


def _rms(t, g):
    tf = t.astype(jnp.float32)
    return tf * lax.rsqrt(jnp.mean(tf * tf, axis=-1, keepdims=True) + NORM_EPS) * g


def _adaln(cvec, w_ada, b_ada):
    return jax.nn.silu(cvec) @ w_ada + b_ada


def _modulate(xs, mod, norm_g):
    shift, scale, gate = jnp.split(mod, 3, axis=-1)
    xn = _rms(xs, norm_g) * (1.0 + scale[..., None, :]) + shift[..., None, :]
    return xn, gate[..., None, :]


def _centred_shift(p, mu):
    zeros = jnp.zeros_like(p[:, :1])
    prev = jnp.concatenate([zeros, p[:, :-1]], axis=1)
    nxt = jnp.concatenate([p[:, 1:], zeros], axis=1)
    return p + mu * (0.5 * (prev + nxt) - p)


def _project(xn, w_in, mu_shift):
    p = jnp.einsum('bld,de->ble', xn, w_in)
    cuts = [int(i) for i in np.cumsum([SHIFT_COLS, RWKV_WIDTH, ATT_WIDTH, KV_WIDTH, KV_WIDTH])]
    sh, z_r, q, k, v, z_a = jnp.split(p, cuts, axis=-1)
    return _centred_shift(sh, mu_shift), z_r, q, k, v, z_a


def _rwkv_prep(sh, w0, w2, a0, a2, k_k, k_a):
    B, L, _ = sh.shape
    sh = sh.astype(jnp.float32)
    r, k, v, lw, la = jnp.split(sh, [RWKV_WIDTH, 2 * RWKV_WIDTH, 3 * RWKV_WIDTH, 3 * RWKV_WIDTH + 2 * LORA], axis=-1)
    heads = lambda t: t.reshape(B, L, RWKV_HEADS, RWKV_HEAD)
    lw = lw.reshape(B, L, 2, LORA)
    la = la.reshape(B, L, 2, LORA)
    w_log = -jax.nn.softplus(-(w0 + jnp.einsum('bldr,drc->bldc', jnp.tanh(lw), w2))) - 0.5
    decay = jnp.exp(-jnp.exp(w_log)).reshape(B, L, 2, RWKV_HEADS, RWKV_HEAD)
    a = jax.nn.sigmoid(a0 + jnp.einsum('bldr,drc->bldc', la, a2)).reshape(B, L, 2, RWKV_HEADS, RWKV_HEAD)
    r, k, v = heads(r), heads(k), heads(v)
    kk = k * k_k.reshape(RWKV_HEADS, RWKV_HEAD)
    kk = kk / jnp.maximum(jnp.linalg.norm(kk, axis=-1, keepdims=True), 1e-12)
    kd = k[:, :, None] * (1.0 + (a - 1.0) * k_a.reshape(RWKV_HEADS, RWKV_HEAD))
    return r, v, kk, decay, a, kd


def _rwkv_scan(S0, r, w, k, v, kk, a, reverse):
    xs = tuple(jnp.moveaxis(t, 1, 0) for t in (r, w, k, v, kk, a))

    def step(S, inp):
        r_t, w_t, k_t, v_t, kk_t, a_t = inp
        s_kk = jnp.einsum('bhvk,bhk->bhv', S, kk_t)
        S = (S * w_t[:, :, None, :] - s_kk[..., None] * (kk_t * a_t)[:, :, None, :]
             + v_t[..., None] * k_t[:, :, None, :])
        return S, jnp.einsum('bhvk,bhk->bhv', S, r_t)

    S, ys = lax.scan(step, S0, xs, reverse=reverse)
    return S, jnp.moveaxis(ys, 0, 1)


def _bidirectional_rwkv(lat, ctx):
    r, v, kk, decay, a, kd = lat
    rc, vc, kkc, decayc, ac, kdc = ctx
    S0 = jnp.zeros((r.shape[0], RWKV_HEADS, RWKV_HEAD, RWKV_HEAD), jnp.float32)
    S_cf, yc_f = _rwkv_scan(S0, rc, decayc[:, :, 0], kdc[:, :, 0], vc, kkc, ac[:, :, 0], False)
    _, yl_f = _rwkv_scan(S_cf, r, decay[:, :, 0], kd[:, :, 0], v, kk, a[:, :, 0], False)
    S_cb, yc_b = _rwkv_scan(S0, rc, decayc[:, :, 1], kdc[:, :, 1], vc, kkc, ac[:, :, 1], True)
    _, yl_b = _rwkv_scan(S_cb, r, decay[:, :, 1], kd[:, :, 1], v, kk, a[:, :, 1], True)
    return yl_f + yl_b, yc_f + yc_b


def _rwkv_output(y, prep, r_k, ln_g, ln_b):
    r, v, _, _, _, kd = prep
    B, L = y.shape[:2]
    mu = jnp.mean(y, axis=-1, keepdims=True)
    var = jnp.mean(jnp.square(y - mu), axis=-1, keepdims=True)
    yn = ((y - mu) * lax.rsqrt(var + GN_EPS)).reshape(B, L, RWKV_WIDTH) * ln_g + ln_b
    bonus = jnp.einsum('blhn,bldhn,hn->blh', r, kd, r_k)[..., None] * v
    return yn + bonus.reshape(B, L, RWKV_WIDTH)


def _axial_rope_tables(L):
    rows = L // GRID_W
    row_ids = jnp.repeat(jnp.arange(rows), GRID_W).astype(jnp.float32)
    col_ids = jnp.tile(jnp.arange(GRID_W), rows).astype(jnp.float32)
    half = HEAD_DIM // 4
    inv = ROPE_THETA ** (-jnp.arange(half, dtype=jnp.float32) / half)
    ang_r = row_ids[:, None] * inv
    ang_c = col_ids[:, None] * inv
    ang = jnp.concatenate([ang_r, ang_r, ang_c, ang_c], axis=-1)
    return jnp.cos(ang), jnp.sin(ang)


def _apply_rope(t, cos, sin):
    parts = t.reshape(*t.shape[:-1], 2, 2, HEAD_DIM // 4)
    rot = jnp.stack([-parts[..., 1, :], parts[..., 0, :]], axis=-2).reshape(t.shape)
    return t * cos[None, :, None, :] + rot * sin[None, :, None, :]


def _sink_softmax(scores, sink):
    m = sink
    for s in scores:
        m = jnp.maximum(m, jnp.max(s, axis=-1, keepdims=True))
    ps = [jnp.exp(s - m) for s in scores]
    denom = jnp.exp(sink - m)
    for p in ps:
        denom = denom + jnp.sum(p, axis=-1, keepdims=True)
    return [p / denom for p in ps]


def _window_attention(q, k, v, kc, vc, sink):
    B, L = q.shape[:2]
    nb = L // BLOCK
    qb = q.reshape(B, nb, BLOCK, KV_HEADS, GROUP, HEAD_DIM) * (HEAD_DIM ** -0.5)
    pad = ((0, 0), (BLOCK, BLOCK), (0, 0), (0, 0))
    kp = jnp.pad(k, pad).reshape(B, nb + 2, BLOCK, KV_HEADS, HEAD_DIM)
    vp = jnp.pad(v, pad).reshape(B, nb + 2, BLOCK, KV_HEADS, HEAD_DIM)
    band = lambda t: jnp.concatenate([t[:, :-2], t[:, 1:-1], t[:, 2:]], axis=2)
    kb, vb = band(kp), band(vp)
    s_lat = jnp.einsum('bnqhgd,bnshd->bhgnqs', qb, kb)
    s_ctx = jnp.einsum('bnqhgd,bshd->bhgnqs', qb, kc)
    qpos = jnp.arange(nb)[:, None, None] * BLOCK + jnp.arange(BLOCK)[None, :, None]
    kpos = (jnp.arange(nb)[:, None, None] - 1) * BLOCK + jnp.arange(3 * BLOCK)[None, None, :]
    valid = (jnp.abs(qpos - kpos) <= WINDOW) & (kpos >= 0) & (kpos < L)
    s_lat = jnp.where(valid, s_lat, -jnp.inf)
    p_lat, p_ctx = _sink_softmax([s_lat, s_ctx], sink.astype(jnp.float32).reshape(1, KV_HEADS, GROUP, 1, 1, 1))
    o = jnp.einsum('bhgnqs,bnshd->bnqhgd', p_lat, vb) + jnp.einsum('bhgnqs,bshd->bnqhgd', p_ctx, vc)
    return o.reshape(B, L, ATT_WIDTH)


def _context_attention(qc, kc, vc, sink):
    B, C = qc.shape[:2]
    qg = qc.reshape(B, C, KV_HEADS, GROUP, HEAD_DIM) * (HEAD_DIM ** -0.5)
    s = jnp.einsum('bqhgd,bshd->bhgqs', qg, kc)
    (p,) = _sink_softmax([s], sink.astype(jnp.float32).reshape(1, KV_HEADS, GROUP, 1, 1))
    return jnp.einsum('bhgqs,bshd->bqhgd', p, vc).reshape(B, C, ATT_WIDTH)


def setup_inputs(seed: int = 0) -> dict:
    key = jax.random.key(seed)
    ks = jax.random.split(key, 24)
    f32 = jnp.float32
    nrm = lambda k, shape, s: jax.random.normal(k, shape, f32) * s
    D = D_MODEL
    ramp = -7.0 + 5.0 * (jnp.arange(RWKV_WIDTH, dtype=f32) / (RWKV_WIDTH - 1)) ** 0.85 + 0.5
    return {
        "x": nrm(ks[0], (BATCH, SEQ, D), 1.0),
        "c": nrm(ks[1], (BATCH, D), 1.0),
        "ctx": nrm(ks[2], (BATCH, CTX_LEN, D), 1.0),
        "c_ctx": nrm(ks[3], (D,), 1.0),
        "w_ada": nrm(ks[4], (DEPTH, D, 3 * D), 0.5 * D ** -0.5),
        "b_ada": nrm(ks[5], (DEPTH, 3 * D), 0.01),
        "norm_g": 1.0 + nrm(ks[6], (DEPTH, D), 0.02),
        "w_in": nrm(ks[7], (DEPTH, D, IN_COLS), D ** -0.5),
        "mu_shift": jax.random.uniform(ks[8], (DEPTH, SHIFT_COLS), f32),
        "w0": ramp + nrm(ks[9], (DEPTH, 2, RWKV_WIDTH), 0.1),
        "w2": nrm(ks[10], (DEPTH, 2, LORA, RWKV_WIDTH), 0.5 * LORA ** -0.5),
        "a0": nrm(ks[11], (DEPTH, 2, RWKV_WIDTH), 0.5),
        "a2": nrm(ks[12], (DEPTH, 2, LORA, RWKV_WIDTH), 0.5 * LORA ** -0.5),
        "k_k": 0.85 + nrm(ks[13], (DEPTH, RWKV_WIDTH), 0.05),
        "k_a": 1.0 + nrm(ks[14], (DEPTH, RWKV_WIDTH), 0.05),
        "r_k": nrm(ks[15], (DEPTH, RWKV_HEADS, RWKV_HEAD), 0.1),
        "ln_x_g": 1.0 + nrm(ks[16], (DEPTH, RWKV_WIDTH), 0.02),
        "ln_x_b": nrm(ks[17], (DEPTH, RWKV_WIDTH), 0.01),
        "q_norm_g": 1.0 + nrm(ks[18], (DEPTH, HEAD_DIM), 0.02),
        "k_norm_g": 1.0 + nrm(ks[19], (DEPTH, HEAD_DIM), 0.02),
        "sink": nrm(ks[20], (DEPTH, N_HEADS), 0.5),
        "w_out": nrm(ks[21], (DEPTH, D, D), D ** -0.5),
    }


def reference(x, c, ctx, c_ctx, w_ada, b_ada, norm_g, w_in, mu_shift, w0, w2, a0, a2, k_k, k_a, r_k,
              ln_x_g, ln_x_b, q_norm_g, k_norm_g, sink, w_out):
    B, L, _ = x.shape
    C = ctx.shape[1]
    cos, sin = _axial_rope_tables(L)
    for l in range(DEPTH):
        mod = _adaln(c, w_ada[l], b_ada[l])
        mod_c = _adaln(c_ctx, w_ada[l], b_ada[l])
        xn, gate = _modulate(x, mod, norm_g[l])
        xcn, gate_c = _modulate(ctx, mod_c, norm_g[l])
        sh, z_r, q, k, v, z_a = _project(xn, w_in[l], mu_shift[l])
        sh_c, z_rc, q_c, k_c, v_c, z_ac = _project(xcn, w_in[l], mu_shift[l])

        prep = _rwkv_prep(sh, w0[l], w2[l], a0[l], a2[l], k_k[l], k_a[l])
        prep_c = _rwkv_prep(sh_c, w0[l], w2[l], a0[l], a2[l], k_k[l], k_a[l])
        y_lat, y_ctx = _bidirectional_rwkv(prep, prep_c)
        rw_out = _rwkv_output(y_lat, prep, r_k[l], ln_x_g[l], ln_x_b[l])

        qh = _apply_rope(_rms(q.reshape(B, L, N_HEADS, HEAD_DIM), q_norm_g[l]), cos, sin)
        kh = _apply_rope(_rms(k.reshape(B, L, KV_HEADS, HEAD_DIM), k_norm_g[l]), cos, sin)
        vh = v.reshape(B, L, KV_HEADS, HEAD_DIM).astype(jnp.float32)
        kch = _rms(k_c.reshape(B, C, KV_HEADS, HEAD_DIM), k_norm_g[l])
        vch = v_c.reshape(B, C, KV_HEADS, HEAD_DIM).astype(jnp.float32)
        att_out = _window_attention(qh, kh, vh, kch, vch, sink[l])

        mixed = jnp.concatenate([rw_out * jax.nn.silu(z_r), att_out * jax.nn.silu(z_a)], axis=-1)
        x_new = x + gate * (mixed @ w_out[l])

        if l + 1 < DEPTH:
            rw_c = _rwkv_output(y_ctx, prep_c, r_k[l], ln_x_g[l], ln_x_b[l])
            qch = _rms(q_c.reshape(B, C, N_HEADS, HEAD_DIM), q_norm_g[l])
            att_c = _context_attention(qch, kch, vch, sink[l])
            mixed_c = jnp.concatenate([rw_c * jax.nn.silu(z_rc), att_c * jax.nn.silu(z_ac)], axis=-1)
            ctx = (ctx + gate_c * (mixed_c @ w_out[l])).astype(ctx.dtype)
        x = x_new.astype(x.dtype)
    return x
```

```python
import functools
import math

import jax
import jax.numpy as jnp
from jax import lax
from jax.experimental import pallas as pl
from jax.experimental.pallas import tpu as pltpu

F32 = jnp.float32
BF16 = jnp.bfloat16

RWKV_WIDTH = 1024
RWKV_HEAD = 64
LORA = 64
GN_EPS = 64e-5
ATT_WIDTH = 1024
HEAD_DIM = 64
N_HEADS = 16
KV_HEADS = 4
GROUP = N_HEADS // KV_HEADS
KV_WIDTH = KV_HEADS * HEAD_DIM
WINDOW = 128
BLOCK = 128
GRID_W = 64
ROPE_THETA = 10000.0
NORM_EPS = 1e-6
SHIFT_COLS = 3 * RWKV_WIDTH + 4 * LORA

LANES = 128
CHUNK = 64
PAIR = LANES
SLAB = 256
VMEM_LIMIT = 56 * 1024 * 1024

NN = (((1,), (0,)), ((), ()))
NT = (((1,), (1,)), ((), ()))


def _dot(a, b, dims=NN):
    return lax.dot_general(a, b, dims, preferred_element_type=F32)


def _split(a):
    hi = a.astype(BF16)
    lo = (a - hi.astype(F32)).astype(BF16)
    return hi, lo


def _dot3s(ap, bp, dims=NN):
    return _dot(ap[0], bp[0], dims) + (_dot(ap[0], bp[1], dims) + _dot(ap[1], bp[0], dims))


def _dot3(a, b, dims=NN):
    return _dot3s(_split(a), _split(b), dims)


def _dot_exact_rhs(a, b_bf16):
    h1 = a.astype(BF16)
    r1 = a - h1.astype(F32)
    h2 = r1.astype(BF16)
    h3 = (r1 - h2.astype(F32)).astype(BF16)
    return _dot(h1, b_bf16) + (_dot(h2, b_bf16) + _dot(h3, b_bf16))


def _dot_exact_lhs(a_bf16, b):
    h1 = b.astype(BF16)
    r1 = b - h1.astype(F32)
    h2 = r1.astype(BF16)
    h3 = (r1 - h2.astype(F32)).astype(BF16)
    return _dot(a_bf16, h1) + (_dot(a_bf16, h2) + _dot(a_bf16, h3))


def _head_ones():
    r = lax.broadcasted_iota(jnp.int32, (LANES, LANES), 0) // RWKV_HEAD
    c = lax.broadcasted_iota(jnp.int32, (LANES, LANES), 1) // RWKV_HEAD
    return jnp.where(r == c, 1.0, 0.0).astype(BF16)


def _segsum(x, bd):
    outs = [_dot_exact_rhs(x[:, s * LANES:(s + 1) * LANES], bd) for s in range(x.shape[1] // LANES)]
    return outs[0] if len(outs) == 1 else jnp.concatenate(outs, axis=1)


def _sigmoid(x):
    return 1.0 / (1.0 + jnp.exp(-x))


def _mod_kernel(c_ref, w_ref, b_ref, o_ref):
    cv = c_ref[...]
    o_ref[...] = _dot3(cv * _sigmoid(cv), w_ref[...]) + b_ref[...]


def _adaln(cmat, w_ada, b_ada):
    rows, d = cmat.shape
    n = w_ada.shape[1]
    tn = 512
    return pl.pallas_call(
        _mod_kernel,
        out_shape=jax.ShapeDtypeStruct((rows, n), F32),
        grid=(n // tn,),
        in_specs=[pl.BlockSpec((rows, d), lambda j: (0, 0)),
                  pl.BlockSpec((d, tn), lambda j: (0, j)),
                  pl.BlockSpec((1, tn), lambda j: (0, j))],
        out_specs=pl.BlockSpec((rows, tn), lambda j: (0, j)),
        compiler_params=pltpu.CompilerParams(dimension_semantics=("arbitrary",), vmem_limit_bytes=VMEM_LIMIT),
        name="adaln_mod",
    )(cmat, w_ada, b_ada.reshape(1, n))


def _norm_kernel(x_ref, g_ref, sc_ref, sh_ref, o_ref):
    xf = x_ref[...]
    ms = jnp.mean(xf * xf, axis=-1, keepdims=True)
    xn = xf * lax.rsqrt(ms + NORM_EPS) * g_ref[...] * (1.0 + sc_ref[0]) + sh_ref[0]
    o_ref[...] = xn.astype(BF16)


def _norm_modulate(x2d, norm_g, mod3, row_of_tile, tm):
    r, d = x2d.shape
    return pl.pallas_call(
        _norm_kernel,
        out_shape=jax.ShapeDtypeStruct((r, d), BF16),
        grid=(r // tm,),
        in_specs=[pl.BlockSpec((tm, d), lambda i: (i, 0)),
                  pl.BlockSpec((1, d), lambda i: (0, 0)),
                  pl.BlockSpec((1, 1, d), lambda i: (row_of_tile(i), 0, 1)),
                  pl.BlockSpec((1, 1, d), lambda i: (row_of_tile(i), 0, 0))],
        out_specs=pl.BlockSpec((tm, d), lambda i: (i, 0)),
        compiler_params=pltpu.CompilerParams(dimension_semantics=("arbitrary",), vmem_limit_bytes=VMEM_LIMIT),
        name="norm_modulate",
    )(x2d, norm_g.reshape(1, d), mod3, mod3)


def _rope_rot(t):
    lane = lax.broadcasted_iota(jnp.int32, t.shape, 1)
    first = (lane % 32) < 16
    return jnp.where(first, -pltpu.roll(t, LANES - 16, 1), pltpu.roll(t, 16, 1))


def _headnorm_rope(t, g, cos, sin, bd, rope, scale):
    ms = _segsum(t * t, bd) * (1.0 / HEAD_DIM)
    tn = t * lax.rsqrt(ms + NORM_EPS) * g
    if rope:
        outs = []
        for s in range(t.shape[1] // LANES):
            ts = tn[:, s * LANES:(s + 1) * LANES]
            outs.append(ts * cos + _rope_rot(ts) * sin)
        tn = jnp.concatenate(outs, axis=1)
    if scale != 1.0:
        tn = tn * scale
    return tn


def _mm_kernel(*refs, mode, rope):
    xn_ref, w_ref = refs[0], refs[1]
    o_ref = refs[-1]
    acc = _dot(xn_ref[...], w_ref[...])
    if mode == "plain":
        o_ref[...] = acc
    elif mode == "silu":
        o_ref[...] = acc * _sigmoid(acc)
    elif mode == "q":
        g_ref, cos_ref, sin_ref = refs[2:5]
        o_ref[...] = _headnorm_rope(acc, g_ref[...], cos_ref[...], sin_ref[...], _head_ones(), rope,
                                    HEAD_DIM ** -0.5)
    elif mode == "kv":
        g_ref, cos_ref, sin_ref = refs[2:5]
        k = _headnorm_rope(acc[:, :KV_WIDTH], g_ref[...], cos_ref[...], sin_ref[...], _head_ones(), rope, 1.0)
        o_ref[:, :KV_WIDTH] = k
        o_ref[:, KV_WIDTH:] = acc[:, KV_WIDTH:]


def _project(xn, w, tm, tn, mode="plain", rope=False, g=None, cos=None, sin=None, pos_tiles=1):
    r, d = xn.shape
    n = w.shape[1]
    in_specs = [pl.BlockSpec((tm, d), lambda i, j: (i, 0)),
                pl.BlockSpec((d, tn), lambda i, j: (0, j))]
    args = [xn, w]
    if mode in ("q", "kv"):
        gw = g.shape[1]
        in_specs += [pl.BlockSpec((1, gw), lambda i, j: (0, 0)),
                     pl.BlockSpec((tm, LANES), lambda i, j: (i % pos_tiles, 0)),
                     pl.BlockSpec((tm, LANES), lambda i, j: (i % pos_tiles, 0))]
        args += [g, cos, sin]
    return pl.pallas_call(
        functools.partial(_mm_kernel, mode=mode, rope=rope),
        out_shape=jax.ShapeDtypeStruct((r, n), F32),
        grid=(r // tm, n // tn),
        in_specs=in_specs,
        out_specs=pl.BlockSpec((tm, tn), lambda i, j: (i, j)),
        compiler_params=pltpu.CompilerParams(dimension_semantics=("arbitrary", "arbitrary"),
                                             vmem_limit_bytes=VMEM_LIMIT),
        name="inproj_" + mode,
    )(*args)


def _prep_kernel(p_ref, hp_ref, hn_ref, mu_ref, w0_ref, w2_ref, a0_ref, a2_ref, kk_ref, ka_ref, rk_ref,
                 v_ref, bon_ref, rt0, kp0, kh0, bh0, g0, rt1, kp1, kh1, bh1, g1, *, nblk, tp):
    i = pl.program_id(1)
    p = p_ref[0]
    rows = lax.broadcasted_iota(jnp.int32, (tp, 1), 0)
    prev_row = hp_ref[0, 7:8, :] * (i > 0).astype(F32)
    next_row = hn_ref[0, 0:1, :] * (i < nblk - 1).astype(F32)
    prev = jnp.where(rows == 0, prev_row, pltpu.roll(p, 1, 0))
    nxt = jnp.where(rows == tp - 1, next_row, pltpu.roll(p, tp - 1, 0))
    sh = p + mu_ref[...] * (0.5 * (prev + nxt) - p)

    w = RWKV_WIDTH
    r, k, v = sh[:, 0:w], sh[:, w:2 * w], sh[:, 2 * w:3 * w]
    lw = jnp.tanh(sh[:, 3 * w:3 * w + 2 * LORA])
    la = sh[:, 3 * w + 2 * LORA:3 * w + 4 * LORA]
    bd = _head_ones()

    kk = k * kk_ref[...]
    n2 = _segsum(kk * kk, bd)
    kap = kk / jnp.maximum(jnp.sqrt(n2), 1e-12)
    v_ref[0] = v

    ri = lax.broadcasted_iota(jnp.int32, (tp, tp), 0)
    ci = lax.broadcasted_iota(jnp.int32, (tp, tp), 1)
    same = (ri // CHUNK) == (ci // CHUNK)
    tri = (jnp.where(same & (ci <= ri), 1.0, 0.0).astype(BF16),
           jnp.where(same & (ci >= ri), 1.0, 0.0).astype(BF16))
    zeros = jnp.zeros((LORA, w), F32)
    lw_p, la_p = _split(lw), _split(la)
    outs = ((rt0, kp0, kh0, bh0, g0), (rt1, kp1, kh1, bh1, g1))
    kd_sum = None
    for d in range(2):
        w2p = jnp.concatenate([w2_ref[d], zeros] if d == 0 else [zeros, w2_ref[d]], axis=0)
        a2p = jnp.concatenate([a2_ref[d], zeros] if d == 0 else [zeros, a2_ref[d]], axis=0)
        wpre = w0_ref[d:d + 1, :] + _dot3s(lw_p, _split(w2p))
        ld = -math.exp(-0.5) * _sigmoid(wpre)
        a = _sigmoid(a0_ref[d:d + 1, :] + _dot3s(la_p, _split(a2p)))
        kd = k * (1.0 + (a - 1.0) * ka_ref[...])
        bb = a * kap
        kd_sum = kd if kd_sum is None else kd_sum + kd
        cum = _dot_exact_lhs(tri[d], ld)
        rt_ref, kp_ref, kh_ref, bh_ref, g_ref = outs[d]
        rt_ref[0] = r * jnp.exp(cum)
        kp_ref[0] = kap * jnp.exp(cum - ld)
        ginv = jnp.exp(-cum)
        kh_ref[0] = kd * ginv
        bh_ref[0] = bb * ginv
        for c in range(tp // CHUNK):
            last = c * CHUNK + (CHUNK - 1 if d == 0 else 0)
            g_ref[0, c] = jnp.broadcast_to(jnp.exp(cum[last:last + 1, :]), (8, w))
    bon_ref[0] = _segsum(r * kd_sum * rk_ref[...], bd) * v


def _rwkv_prep(sh3, mu, w0, w2, a0, a2, k_k, k_a, r_k):
    b, l, sc = sh3.shape
    tp = 128
    nblk = l // tp
    w = RWKV_WIDTH
    full = lambda shape: pl.BlockSpec(shape, lambda bi, i: (0,) * len(shape))
    tok = pl.BlockSpec((1, tp, w), lambda bi, i: (bi, i, 0))
    gsp = pl.BlockSpec((1, tp // CHUNK, 8, w), lambda bi, i: (bi, i, 0, 0))
    tok_shape = jax.ShapeDtypeStruct((b, l, w), F32)
    g_shape = jax.ShapeDtypeStruct((b, l // CHUNK, 8, w), F32)
    return pl.pallas_call(
        functools.partial(_prep_kernel, nblk=nblk, tp=tp),
        out_shape=[tok_shape, tok_shape] + [tok_shape] * 4 + [g_shape] + [tok_shape] * 4 + [g_shape],
        grid=(b, nblk),
        in_specs=[pl.BlockSpec((1, tp, sc), lambda bi, i: (bi, i, 0)),
                  pl.BlockSpec((1, 8, sc), lambda bi, i: (bi, jnp.maximum(i * (tp // 8) - 1, 0), 0)),
                  pl.BlockSpec((1, 8, sc), lambda bi, i: (bi, jnp.minimum((i + 1) * (tp // 8), l // 8 - 1), 0)),
                  full((1, sc)), full((2, w)), full((2, LORA, w)), full((2, w)), full((2, LORA, w)),
                  full((1, w)), full((1, w)), full((1, w))],
        out_specs=[tok, tok] + [tok] * 4 + [gsp] + [tok] * 4 + [gsp],
        compiler_params=pltpu.CompilerParams(dimension_semantics=("arbitrary", "arbitrary"),
                                             vmem_limit_bytes=VMEM_LIMIT),
        name="rwkv_prep",
    )(sh3, sh3, sh3, mu.reshape(1, sc), w0, w2, a0, a2, k_k.reshape(1, w), k_a.reshape(1, w), r_k.reshape(1, w))


def _stack_heads(x, first_head):
    z = jnp.zeros_like(x)
    return jnp.concatenate([jnp.where(first_head, x, z), jnp.where(first_head, z, x)], axis=0)


def _chunk_step(s0, rt, kp, kh, bh, v, gt, d, emit_y):
    t2 = 2 * CHUNK
    lane = lax.broadcasted_iota(jnp.int32, (CHUNK, PAIR), 1)
    first_head = lane < RWKV_HEAD
    ri = lax.broadcasted_iota(jnp.int32, (t2, t2), 0)
    ci = lax.broadcasted_iota(jnp.int32, (t2, t2), 1)
    ti, tj = ri % CHUNK, ci % CHUNK
    strict = (tj < ti) if d == 0 else (tj > ti)
    incl = (tj <= ti) if d == 0 else (tj >= ti)
    blk = (ri // 16) == (ci // 16)
    eye = jnp.where(ri == ci, 1.0, 0.0)

    kp_s = _split(_stack_heads(kp, first_head))
    kh_s = _split(_stack_heads(kh, first_head))
    bh_s = _split(_stack_heads(bh, first_head))
    v_st = _stack_heads(v, first_head)
    v_s = _split(v_st)

    ab = jnp.where(strict, _dot3s(kp_s, bh_s, NT), 0.0)
    ak = jnp.where(strict, _dot3s(kp_s, kh_s, NT), 0.0)

    ad = jnp.where(blk, ab, 0.0)
    aoff = ab - ad
    ad_s = _split(ad)
    a2 = _dot3s(ad_s, ad_s)
    a2_s = _split(a2)
    a4 = _dot3s(a2_s, a2_s)
    a4_s = _split(a4)
    a8 = _dot3s(a4_s, a4_s)
    x = eye - ad
    x = x + _dot3s(_split(x), a2_s)
    x = x + _dot3s(_split(x), a4_s)
    dm = x + _dot3s(_split(x), _split(a8))
    dm_s = _split(dm)
    n = _dot3s(dm_s, _split(aoff))
    n_s = _split(n)
    n2 = _dot3s(n_s, n_s)
    z = eye - n
    z = z + _dot3s(_split(z), _split(n2))
    z_s = _split(z)

    akv = _dot3s(_split(ak), v_s)
    w1 = _dot3s(z_s, _split(_dot3s(dm_s, kp_s)))
    u0 = _dot3s(z_s, _split(_dot3s(dm_s, _split(akv))))

    s0_s = _split(s0)
    u = _dot3s(_split(w1), s0_s, NT) + u0
    u_s = _split(u)
    kt_s = _split(_stack_heads(kh * gt, first_head))
    bt_s = _split(_stack_heads(bh * gt, first_head))
    s_new = s0 * gt + _dot3s(_split(v_st.T), kt_s) - _dot3s(_split(u.T), bt_s)
    if not emit_y:
        return s_new, None
    r_s = _split(_stack_heads(rt, first_head))
    bk = jnp.where(incl, _dot3s(r_s, kh_s, NT), 0.0)
    bb = jnp.where(incl, _dot3s(r_s, bh_s, NT), 0.0)
    y_st = _dot3s(r_s, s0_s, NT) + _dot3s(_split(bk), v_s) - _dot3s(_split(bb), u_s)
    return s_new, y_st[:CHUNK] + y_st[CHUNK:]


def _scan_kernel(*refs, emit_y, npair):
    ins = refs[:13]
    (rt0, kp0, kh0, bh0, g0, v0, rt1, kp1, kh1, bh1, g1, v1, sinit) = ins
    if emit_y:
        y0, y1, sout, s_scr = refs[13:]
    else:
        sout, s_scr = refs[13:]
        y0 = y1 = None
    i = pl.program_id(2)

    @pl.when(i == 0)
    def _():
        s_scr[...] = sinit[0]

    dirs = ((rt0, kp0, kh0, bh0, g0, v0, y0), (rt1, kp1, kh1, bh1, g1, v1, y1))
    for d in range(2):
        rt, kp, kh, bh, g, v, y = dirs[d]
        for pp in range(npair):
            ls = slice(pp * PAIR, (pp + 1) * PAIR)
            s_new, yv = _chunk_step(s_scr[d, pp], rt[0, :, ls], kp[0, :, ls], kh[0, :, ls], bh[0, :, ls],
                                    v[0, :, ls], g[0, 0, 0:1, ls], d, emit_y)
            s_scr[d, pp] = s_new
            if emit_y:
                y[0, :, ls] = yv

    @pl.when(i == pl.num_programs(2) - 1)
    def _():
        sout[0] = s_scr[...]


def _rwkv_scan(prep, s_init, emit_y):
    v, _, rt0, kp0, kh0, bh0, g0, rt1, kp1, kh1, bh1, g1 = prep
    b, l, w = v.shape
    nc = l // CHUNK
    ns = w // SLAB
    npair = SLAB // PAIR
    fwd = lambda bi, s, i: (bi, i, s)
    bwd = lambda bi, s, i: (bi, nc - 1 - i, s)
    tok_f = pl.BlockSpec((1, CHUNK, SLAB), fwd)
    tok_b = pl.BlockSpec((1, CHUNK, SLAB), bwd)
    g_f = pl.BlockSpec((1, 1, 8, SLAB), lambda bi, s, i: (bi, i, 0, s))
    g_b = pl.BlockSpec((1, 1, 8, SLAB), lambda bi, s, i: (bi, nc - 1 - i, 0, s))
    st = pl.BlockSpec((1, 2, npair, PAIR, PAIR), lambda bi, s, i: (bi, 0, s, 0, 0))
    st_shape = jax.ShapeDtypeStruct(s_init.shape, F32)
    y_shape = jax.ShapeDtypeStruct((b, l, w), F32)
    out_shape = [y_shape, y_shape, st_shape] if emit_y else [st_shape]
    out_specs = [tok_f, tok_b, st] if emit_y else [st]
    return pl.pallas_call(
        functools.partial(_scan_kernel, emit_y=emit_y, npair=npair),
        out_shape=out_shape,
        grid=(b, ns, nc),
        in_specs=[tok_f] * 4 + [g_f, tok_f] + [tok_b] * 4 + [g_b, tok_b] + [st],
        out_specs=out_specs,
        scratch_shapes=[pltpu.VMEM((2, npair, PAIR, PAIR), F32)],
        compiler_params=pltpu.CompilerParams(dimension_semantics=("arbitrary", "arbitrary", "arbitrary"),
                                             vmem_limit_bytes=VMEM_LIMIT),
        name="rwkv_scan_y" if emit_y else "rwkv_scan_state",
    )(rt0, kp0, kh0, bh0, g0, v, rt1, kp1, kh1, bh1, g1, v, s_init)


def _attn_kernel(q_ref, kvp_ref, kvc_ref, kvn_ref, kvx_ref, za_ref, sink_ref, o_ref, *, nb):
    i = pl.program_id(1)
    q = q_ref[0]
    kv = jnp.concatenate([kvp_ref[0], kvc_ref[0], kvn_ref[0]], axis=0)
    kvx = kvx_ref[0]
    a = lax.broadcasted_iota(jnp.int32, (BLOCK, 3 * BLOCK), 0)
    c = lax.broadcasted_iota(jnp.int32, (BLOCK, 3 * BLOCK), 1)
    rel = c - BLOCK - a
    lo = jnp.where(i > 0, 0, BLOCK)
    hi = jnp.where(i < nb - 1, 3 * BLOCK, 2 * BLOCK)
    valid = (jnp.abs(rel) <= WINDOW) & (c >= lo) & (c < hi)
    sink = sink_ref[...]
    outs = []
    for g in range(KV_HEADS):
        ks = slice(g * HEAD_DIM, (g + 1) * HEAD_DIM)
        vs = slice(KV_WIDTH + g * HEAD_DIM, KV_WIDTH + (g + 1) * HEAD_DIM)
        k_lat, v_lat = kv[:, ks].astype(BF16), kv[:, vs].astype(BF16)
        k_ctx, v_ctx = kvx[:, ks].astype(BF16), kvx[:, vs].astype(BF16)
        for j in range(GROUP):
            h = g * GROUP + j
            qh = q[:, h * HEAD_DIM:(h + 1) * HEAD_DIM].astype(BF16)
            s_lat = jnp.where(valid, _dot(qh, k_lat, NT), -jnp.inf)
            s_ctx = _dot(qh, k_ctx, NT)
            sk = sink[:, h:h + 1]
            m = jnp.maximum(sk, jnp.maximum(jnp.max(s_lat, axis=-1, keepdims=True),
                                            jnp.max(s_ctx, axis=-1, keepdims=True)))
            p_lat = jnp.exp(s_lat - m)
            p_ctx = jnp.exp(s_ctx - m)
            denom = jnp.exp(sk - m) + jnp.sum(p_lat, axis=-1, keepdims=True) + jnp.sum(p_ctx, axis=-1, keepdims=True)
            o = _dot(p_lat.astype(BF16), v_lat) + _dot(p_ctx.astype(BF16), v_ctx)
            outs.append(o / denom)
    o_ref[0] = (jnp.concatenate(outs, axis=1) * za_ref[0]).astype(BF16)


def _attention(q3, kv3, kvx3, za3, sink):
    b, l, _ = q3.shape
    c = kvx3.shape[1]
    nb = l // BLOCK
    kvw = 2 * KV_WIDTH
    return pl.pallas_call(
        functools.partial(_attn_kernel, nb=nb),
        out_shape=jax.ShapeDtypeStruct((b, l, ATT_WIDTH), BF16),
        grid=(b, nb),
        in_specs=[pl.BlockSpec((1, BLOCK, ATT_WIDTH), lambda bi, i: (bi, i, 0)),
                  pl.BlockSpec((1, BLOCK, kvw), lambda bi, i: (bi, jnp.maximum(i - 1, 0), 0)),
                  pl.BlockSpec((1, BLOCK, kvw), lambda bi, i: (bi, i, 0)),
                  pl.BlockSpec((1, BLOCK, kvw), lambda bi, i: (bi, jnp.minimum(i + 1, nb - 1), 0)),
                  pl.BlockSpec((1, c, kvw), lambda bi, i: (bi, 0, 0)),
                  pl.BlockSpec((1, BLOCK, ATT_WIDTH), lambda bi, i: (bi, i, 0)),
                  pl.BlockSpec((1, N_HEADS), lambda bi, i: (0, 0))],
        out_specs=pl.BlockSpec((1, BLOCK, ATT_WIDTH), lambda bi, i: (bi, i, 0)),
        compiler_params=pltpu.CompilerParams(dimension_semantics=("arbitrary", "arbitrary"),
                                             vmem_limit_bytes=VMEM_LIMIT),
        name="window_attention",
    )(q3, kv3, kv3, kv3, kvx3, za3, sink.reshape(1, N_HEADS))


def _out_kernel(y0_ref, y1_ref, bon_ref, zr_ref, att_ref, lng_ref, lnb_ref, x_ref, gate_ref, w_ref, o_ref):
    bd = _head_ones()
    y = y0_ref[...] + y1_ref[...]
    mu = _segsum(y, bd) * (1.0 / RWKV_HEAD)
    yc = y - mu
    var = _segsum(yc * yc, bd) * (1.0 / RWKV_HEAD)
    yn = yc * lax.rsqrt(var + GN_EPS) * lng_ref[...] + lnb_ref[...]
    rw = ((yn + bon_ref[...]) * zr_ref[...]).astype(BF16)
    mixed = jnp.concatenate([rw, att_ref[...]], axis=1)
    o_ref[...] = x_ref[...] + gate_ref[0] * _dot(mixed, w_ref[...])


def _out_stage(y0, y1, bon, zr, att, ln_g, ln_b, x2d, mod3, w_out, rows_per_batch):
    r, d = x2d.shape
    tm = 256
    w = RWKV_WIDTH
    tok = pl.BlockSpec((tm, w), lambda i: (i, 0))
    vec = pl.BlockSpec((1, w), lambda i: (0, 0))
    return pl.pallas_call(
        _out_kernel,
        out_shape=jax.ShapeDtypeStruct((r, d), F32),
        grid=(r // tm,),
        in_specs=[tok, tok, tok, tok, tok, vec, vec,
                  pl.BlockSpec((tm, d), lambda i: (i, 0)),
                  pl.BlockSpec((1, 1, d), lambda i: ((i * tm) // rows_per_batch, 0, 2)),
                  pl.BlockSpec((d, d), lambda i: (0, 0))],
        out_specs=pl.BlockSpec((tm, d), lambda i: (i, 0)),
        compiler_params=pltpu.CompilerParams(dimension_semantics=("arbitrary",), vmem_limit_bytes=VMEM_LIMIT),
        name="out_stage",
    )(y0, y1, bon, zr, att, ln_g.reshape(1, w), ln_b.reshape(1, w), x2d, mod3, w_out)


def _rope_tables(l):
    rows = l // GRID_W
    row_ids = jnp.repeat(jnp.arange(rows), GRID_W).astype(F32)
    col_ids = jnp.tile(jnp.arange(GRID_W), rows).astype(F32)
    half = HEAD_DIM // 4
    inv = ROPE_THETA ** (-jnp.arange(half, dtype=F32) / half)
    ang_r = row_ids[:, None] * inv
    ang_c = col_ids[:, None] * inv
    ang = jnp.concatenate([ang_r, ang_r, ang_c, ang_c], axis=-1)
    return jnp.tile(jnp.cos(ang), (1, 2)), jnp.tile(jnp.sin(ang), (1, 2))


def kernel(x, c, ctx, c_ctx, w_ada, b_ada, norm_g, w_in, mu_shift, w0, w2, a0, a2, k_k, k_a, r_k,
           ln_x_g, ln_x_b, q_norm_g, k_norm_g, sink, w_out):
    b, l, d = x.shape
    cl = ctx.shape[1]
    depth = w_ada.shape[0]
    assert depth == 1, "the context-stream update between layers is not implemented"
    cos, sin = _rope_tables(l)
    w = RWKV_WIDTH
    seg = [0, SHIFT_COLS, SHIFT_COLS + w, SHIFT_COLS + w + ATT_WIDTH,
           SHIFT_COLS + w + ATT_WIDTH + 2 * KV_WIDTH, SHIFT_COLS + w + 2 * ATT_WIDTH + 2 * KV_WIDTH]
    for layer in range(depth):
        cmat = jnp.concatenate([c, c_ctx[None, :], jnp.zeros((8 - b - 1, d), F32)], axis=0)
        mod3 = _adaln(cmat, w_ada[layer], b_ada[layer]).reshape(8, 1, 3 * d)

        tm = min(1024, l)
        xn = _norm_modulate(x.reshape(b * l, d), norm_g[layer], mod3, lambda i: (i * tm) // l, tm)
        xcn = _norm_modulate(ctx.reshape(b * cl, d), norm_g[layer], mod3, lambda i: b, cl)

        wl = w_in[layer].astype(BF16)
        w_sh, w_zr, w_q, w_kv, w_za = (wl[:, seg[s]:seg[s + 1]] for s in range(5))
        gq = jnp.tile(q_norm_g[layer], N_HEADS).reshape(1, ATT_WIDTH)
        gk = jnp.tile(k_norm_g[layer], KV_HEADS).reshape(1, KV_WIDTH)

        sh = _project(xn, w_sh, tm, SHIFT_COLS // 2)
        zr = _project(xn, w_zr, tm, w, mode="silu")
        q = _project(xn, w_q, tm, ATT_WIDTH, mode="q", rope=True, g=gq, cos=cos, sin=sin, pos_tiles=l // tm)
        kv = _project(xn, w_kv, tm, 2 * KV_WIDTH, mode="kv", rope=True, g=gk, cos=cos, sin=sin, pos_tiles=l // tm)
        za = _project(xn, w_za, tm, ATT_WIDTH, mode="silu")
        sh_c = _project(xcn, w_sh, cl, SHIFT_COLS // 2)
        kv_c = _project(xcn, w_kv, cl, 2 * KV_WIDTH, mode="kv", rope=False, g=gk, cos=cos, sin=sin, pos_tiles=1)

        rwkv_args = (mu_shift[layer], w0[layer], w2[layer], a0[layer], a2[layer], k_k[layer], k_a[layer],
                     r_k[layer].reshape(-1))
        prep_c = _rwkv_prep(sh_c.reshape(b, cl, SHIFT_COLS), *rwkv_args)
        prep = _rwkv_prep(sh.reshape(b, l, SHIFT_COLS), *rwkv_args)
        s_zero = jnp.zeros((b, 2, w // PAIR, PAIR, PAIR), F32)
        (s_ctx,) = _rwkv_scan(prep_c, s_zero, emit_y=False)
        y0, y1, _ = _rwkv_scan(prep, s_ctx, emit_y=True)

        att = _attention(q.reshape(b, l, ATT_WIDTH), kv.reshape(b, l, 2 * KV_WIDTH),
                         kv_c.reshape(b, cl, 2 * KV_WIDTH), za.reshape(b, l, ATT_WIDTH), sink[layer])

        x_new = _out_stage(y0.reshape(b * l, w), y1.reshape(b * l, w), prep[1].reshape(b * l, w), zr,
                           att.reshape(b * l, ATT_WIDTH), ln_x_g[layer], ln_x_b[layer],
                           x.reshape(b * l, d), mod3, w_out[layer].astype(BF16), l)
        x = x_new.reshape(b, l, d)
    return x
```
